```python
import math
import jax, jax.numpy as jnp
from jax import lax
import numpy as np

D_MODEL = 2048
BATCH = 1
SEQ = 16384
DEPTH = 4
DEC_BATCH = 2
DEC_SEQ = 4096
PAST_LEN = 128

N_MIXERS = 2
N_ATTN_LAYERS = (DEPTH + 1) // 2
N_DN_LAYERS = DEPTH // 2
RMS_EPS = 1e-6
L2_EPS = 1e-6

ATTN_HEADS = 8
ATTN_HEAD_DIM = D_MODEL // ATTN_HEADS // 2
ATTN_V_DIM = 2 * ATTN_HEAD_DIM
ATTN_QK_W = ATTN_HEADS * 2 * ATTN_HEAD_DIM
ATTN_INNER = ATTN_HEADS * ATTN_V_DIM
ATTN_IN = 2 * ATTN_QK_W + 2 * ATTN_INNER
Q_BLOCK = 128

DN_K_HEADS = 16
DN_V_HEADS = 32
DN_DK = 128
DN_DV = 128
DN_KW = DN_K_HEADS * DN_DK
DN_VW = DN_V_HEADS * DN_DV
DN_CONV_DIM = 2 * DN_KW + DN_VW
DN_IN = DN_CONV_DIM + DN_VW + 4 * DN_V_HEADS
CONV_W = 5
CHUNK = 64

kernel_name = 'hybrid_diffattn_gdn_bidir_encoder'


def rms_norm(x, g):
    xf = x.astype(jnp.float32)
    y = xf * lax.rsqrt(jnp.mean(xf * xf, axis=-1, keepdims=True) + RMS_EPS)
    return (y * g.astype(jnp.float32)).astype(x.dtype)


def l2_norm(x):
    xf = x.astype(jnp.float32)
    return xf * lax.rsqrt(jnp.sum(xf * xf, axis=-1, keepdims=True) + L2_EPS)


def alibi_slopes(n_heads):
    return jnp.exp2(-8.0 * jnp.arange(1, n_heads + 1, dtype=jnp.float32) / n_heads)


def lambda_init_fn(layer_idx):
    return 0.8 - 0.6 * math.exp(-0.3 * layer_idx)


def diff_softmax_attention(q1, q2, k1, k2, v, lam):
    B, T, H, dh = q1.shape
    dv = v.shape[-1]
    nb = T // Q_BLOCK
    scale = dh ** -0.5
    slopes = alibi_slopes(H)
    pos_k = jnp.arange(T, dtype=jnp.float32)

    def to_blocks(a):
        return jnp.moveaxis(a.reshape(B, nb, Q_BLOCK, H, dh), 1, 0)

    starts = jnp.arange(nb, dtype=jnp.float32) * Q_BLOCK

    def one_block(args):
        qa, qb, start = args
        pos_q = start + jnp.arange(Q_BLOCK, dtype=jnp.float32)
        bias = -slopes[:, None, None] * jnp.abs(pos_q[:, None] - pos_k[None, :])[None]
        s1 = jnp.einsum('bqhd,bkhd->bhqk', qa, k1).astype(jnp.float32) * scale + bias
        s2 = jnp.einsum('bqhd,bkhd->bhqk', qb, k2).astype(jnp.float32) * scale + bias
        a = jax.nn.softmax(s1, axis=-1) - lam * jax.nn.softmax(s2, axis=-1)
        return jnp.einsum('bhqk,bkhe->bqhe', a.astype(v.dtype), v)

    o = lax.map(one_block, (to_blocks(q1), to_blocks(q2), starts))
    return jnp.moveaxis(o, 0, 1).reshape(B, T, H, dv)


def diff_attention_mixer(h, w_in, lam_vec, subln_g, w_out, lambda_init):
    B, T, _ = h.shape
    proj = h @ w_in
    q = proj[..., :ATTN_QK_W].reshape(B, T, ATTN_HEADS, 2, ATTN_HEAD_DIM)
    k = proj[..., ATTN_QK_W:2 * ATTN_QK_W].reshape(B, T, ATTN_HEADS, 2, ATTN_HEAD_DIM)
    v = proj[..., 2 * ATTN_QK_W:2 * ATTN_QK_W + ATTN_INNER].reshape(B, T, ATTN_HEADS, ATTN_V_DIM)
    gate = proj[..., 2 * ATTN_QK_W + ATTN_INNER:]
    lv = lam_vec.astype(jnp.float32)
    lam = jnp.exp(jnp.sum(lv[0] * lv[1])) - jnp.exp(jnp.sum(lv[2] * lv[3])) + lambda_init
    o = diff_softmax_attention(q[..., 0, :], q[..., 1, :], k[..., 0, :], k[..., 1, :], v, lam)
    o = rms_norm(o, subln_g) * (1.0 - lambda_init)
    o = o.reshape(B, T, ATTN_INNER) * jax.nn.silu(gate)
    return o @ w_out


def short_conv(x, w):
    C = x.shape[-1]
    return lax.conv_general_dilated(
        x, w.reshape(CONV_W, 1, C).astype(x.dtype), window_strides=(1,),
        padding=[(CONV_W // 2, CONV_W // 2)], dimension_numbers=('NWC', 'WIO', 'NWC'),
        feature_group_count=C)


def gated_delta_rule(q, k, v, g, beta):
    B, H, T, dk = q.shape
    dv = v.shape[-1]
    N = T // CHUNK
    q = q * dk ** -0.5

    def rs(a):
        return a.reshape(B, H, N, CHUNK, *a.shape[3:])

    q, k, v, g, beta = rs(q), rs(k), rs(v), rs(g), rs(beta)
    g = jnp.cumsum(g, axis=-1)
    idx = jnp.arange(CHUNK)
    tril = idx[:, None] >= idx[None, :]
    strict = idx[:, None] > idx[None, :]
    decay = jnp.exp(jnp.where(tril, g[..., :, None] - g[..., None, :], -jnp.inf))
    k_beta = k * beta[..., None]
    v_beta = v * beta[..., None]
    L = jnp.where(strict, jnp.einsum('bhncd,bhnsd->bhncs', k_beta, k) * decay, 0.0)
    Tm = jnp.eye(CHUNK, dtype=jnp.float32) + L
    u = lax.linalg.triangular_solve(Tm, v_beta, left_side=True, lower=True)
    w = lax.linalg.triangular_solve(Tm, k_beta * jnp.exp(g)[..., None], left_side=True, lower=True)
    a_intra = jnp.where(tril, jnp.einsum('bhncd,bhnsd->bhncs', q, k) * decay, 0.0)
    g_last = g[..., -1]
    q_dec = q * jnp.exp(g)[..., None]
    k_dec = k * jnp.exp(g_last[..., None] - g)[..., None]

    def step(S, inp):
        qd, kd, u_c, w_c, a_c, gl = inp
        v_new = u_c - jnp.einsum('bhcd,bhde->bhce', w_c, S)
        o = jnp.einsum('bhcd,bhde->bhce', qd, S) + jnp.einsum('bhcs,bhse->bhce', a_c, v_new)
        S = S * jnp.exp(gl)[..., None, None] + jnp.einsum('bhcd,bhce->bhde', kd, v_new)
        return S, o

    xs = tuple(jnp.moveaxis(a, 2, 0) for a in (q_dec, k_dec, u, w, a_intra, g_last))
    S0 = jnp.zeros((B, H, dk, dv), jnp.float32)
    _, o = lax.scan(step, S0, xs)
    return jnp.moveaxis(o, 0, 2).reshape(B, H, T, dv)


def gated_deltanet_mixer(h, w_in, conv_w, a_log_f, dt_bias_f, a_log_b, dt_bias_b, norm_g, w_out):
    B, T, _ = h.shape
    proj = h @ w_in
    qkv = jax.nn.silu(short_conv(proj[..., :DN_CONV_DIM], conv_w))
    z = proj[..., DN_CONV_DIM:DN_CONV_DIM + DN_VW].reshape(B, T, DN_V_HEADS, DN_DV)
    ab = proj[..., DN_CONV_DIM + DN_VW:].astype(jnp.float32).reshape(B, T, 4, DN_V_HEADS)
    rep = DN_V_HEADS // DN_K_HEADS
    q = jnp.repeat(l2_norm(qkv[..., :DN_KW].reshape(B, T, DN_K_HEADS, DN_DK)), rep, axis=2)
    k = jnp.repeat(l2_norm(qkv[..., DN_KW:2 * DN_KW].reshape(B, T, DN_K_HEADS, DN_DK)), rep, axis=2)
    v = qkv[..., 2 * DN_KW:].reshape(B, T, DN_V_HEADS, DN_DV).astype(jnp.float32)
    q, k, v = (jnp.transpose(a, (0, 2, 1, 3)) for a in (q, k, v))

    def gates(a, b, a_log, dt_bias):
        g = -jnp.exp(a_log.astype(jnp.float32)) * jax.nn.softplus(a + dt_bias.astype(jnp.float32))
        return jnp.transpose(g, (0, 2, 1)), jnp.transpose(jax.nn.sigmoid(b), (0, 2, 1))

    g_f, beta_f = gates(ab[:, :, 0], ab[:, :, 1], a_log_f, dt_bias_f)
    g_b, beta_b = gates(ab[:, :, 2], ab[:, :, 3], a_log_b, dt_bias_b)

    def flip(a):
        return jnp.flip(a, axis=2)

    o_f = gated_delta_rule(q, k, v, g_f, beta_f)
    o_b = flip(gated_delta_rule(flip(q), flip(k), flip(v), flip(g_b), flip(beta_b)))
    o = jnp.transpose(o_f + o_b, (0, 2, 1, 3))
    o = rms_norm(o, norm_g) * jax.nn.silu(z.astype(jnp.float32))
    return o.reshape(B, T, DN_VW).astype(h.dtype) @ w_out


def trunk(x, norm_g, attn_w_in, attn_lambda, attn_subln_g, attn_w_out,
          dn_w_in, dn_conv_w, dn_a_log_fwd, dn_dt_bias_fwd, dn_a_log_bwd, dn_dt_bias_bwd,
          dn_norm_g, dn_w_out, final_norm_g):
    for i in range(DEPTH):
        h = rms_norm(x, norm_g[i])
        j = i // N_MIXERS
        if i % N_MIXERS == 0:
            y = diff_attention_mixer(h, attn_w_in[j], attn_lambda[j], attn_subln_g[j],
                                     attn_w_out[j], lambda_init_fn(i))
        else:
            y = gated_deltanet_mixer(h, dn_w_in[j], dn_conv_w[j], dn_a_log_fwd[j], dn_dt_bias_fwd[j],
                                     dn_a_log_bwd[j], dn_dt_bias_bwd[j], dn_norm_g[j], dn_w_out[j])
        x = x + y.astype(x.dtype)
    return rms_norm(x, final_norm_g)


def setup_inputs(seed: int = 0) -> dict:
    key = jax.random.key(seed)
    ks = jax.random.split(key, 20)
    f32 = jnp.float32
    nrm = lambda k, s: jax.random.normal(k, s, f32)

    def dt_bias(k):
        dt = jnp.exp(jax.random.uniform(k, (N_DN_LAYERS, DN_V_HEADS), f32, math.log(1e-3), math.log(1e-1)))
        return dt + jnp.log(-jnp.expm1(-dt))

    return {
        'x_prompt': nrm(ks[0], (BATCH, SEQ, D_MODEL)),
        'x_sample': nrm(ks[1], (DEC_BATCH, DEC_SEQ, D_MODEL)),
        'norm_g': 1.0 + 0.02 * nrm(ks[2], (DEPTH, D_MODEL)),
        'attn_w_in': nrm(ks[3], (N_ATTN_LAYERS, D_MODEL, ATTN_IN)) * D_MODEL ** -0.5,
        'attn_lambda': 0.1 * nrm(ks[4], (N_ATTN_LAYERS, 4, ATTN_HEAD_DIM)),
        'attn_subln_g': 1.0 + 0.02 * nrm(ks[5], (N_ATTN_LAYERS, ATTN_V_DIM)),
        'attn_w_out': nrm(ks[6], (N_ATTN_LAYERS, ATTN_INNER, D_MODEL)) * ATTN_INNER ** -0.5,
        'dn_w_in': nrm(ks[7], (N_DN_LAYERS, D_MODEL, DN_IN)) * D_MODEL ** -0.5,
        'dn_conv_w': nrm(ks[8], (N_DN_LAYERS, CONV_W, DN_CONV_DIM)) * CONV_W ** -0.5,
        'dn_a_log_fwd': jnp.log(jax.random.uniform(ks[9], (N_DN_LAYERS, DN_V_HEADS), f32, 1.0, 16.0)),
        'dn_dt_bias_fwd': dt_bias(ks[10]),
        'dn_a_log_bwd': jnp.log(jax.random.uniform(ks[11], (N_DN_LAYERS, DN_V_HEADS), f32, 1.0, 16.0)),
        'dn_dt_bias_bwd': dt_bias(ks[12]),
        'dn_norm_g': 1.0 + 0.02 * nrm(ks[13], (N_DN_LAYERS, DN_DV)),
        'dn_w_out': nrm(ks[14], (N_DN_LAYERS, DN_VW, D_MODEL)) * DN_VW ** -0.5,
        'final_norm_g': 1.0 + 0.02 * nrm(ks[15], (D_MODEL,)),
    }


def reference(x_prompt, x_sample, norm_g, attn_w_in, attn_lambda, attn_subln_g, attn_w_out,
              dn_w_in, dn_conv_w, dn_a_log_fwd, dn_dt_bias_fwd, dn_a_log_bwd, dn_dt_bias_bwd,
              dn_norm_g, dn_w_out, final_norm_g):
    y_prompt = trunk(x_prompt, norm_g, attn_w_in, attn_lambda, attn_subln_g, attn_w_out,
                     dn_w_in, dn_conv_w, dn_a_log_fwd, dn_dt_bias_fwd, dn_a_log_bwd, dn_dt_bias_bwd,
                     dn_norm_g, dn_w_out, final_norm_g)
    y_sample = trunk(x_sample, norm_g, attn_w_in, attn_lambda, attn_subln_g, attn_w_out,
                     dn_w_in, dn_conv_w, dn_a_log_fwd, dn_dt_bias_fwd, dn_a_log_bwd, dn_dt_bias_bwd,
                     dn_norm_g, dn_w_out, final_norm_g)
    return (y_prompt, y_sample)
```

```python
import functools
import math

import jax
import jax.numpy as jnp
from jax import lax
from jax.experimental import pallas as pl
from jax.experimental.pallas import tpu as pltpu

F32 = jnp.float32
BF16 = jnp.bfloat16

D_MODEL = 2048
DEPTH = 4
RMS_EPS = 1e-6
L2_EPS = 1e-6

ATTN_HEADS = 8
ATTN_HEAD_DIM = 128
ATTN_V_DIM = 256
ATTN_IN = 8192

DN_K_HEADS = 16
DN_V_HEADS = 32
DN_DK = 128
DN_DV = 128
DN_KW = 2048
DN_VW = 4096
DN_CONV_DIM = 8192
DN_MAIN = DN_CONV_DIM + DN_VW
CONV_W = 5
CHUNK = 64

LANES = 128
VMEM_LIMIT = 56 * 1024 * 1024
NEG_BIG = -1e30

NT_DIMS = (((1,), (1,)), ((), ()))
TN_DIMS = (((0,), (0,)), ((), ()))


def _params(*sem):
    return pltpu.CompilerParams(dimension_semantics=sem, vmem_limit_bytes=VMEM_LIMIT)


def _sigmoid(x):
    return 1.0 / (1.0 + jnp.exp(-x))


def _rmsnorm_kernel(x_ref, g_ref, o_ref):
    x = x_ref[...]
    y = x * lax.rsqrt(jnp.mean(x * x, axis=-1, keepdims=True) + RMS_EPS)
    o_ref[...] = (y * g_ref[...]).astype(o_ref.dtype)


def rmsnorm(x, g, tm=512):
    M, D = x.shape
    tm = min(tm, M)
    return pl.pallas_call(
        _rmsnorm_kernel,
        grid=(M // tm,),
        in_specs=[pl.BlockSpec((tm, D), lambda i: (i, 0)), pl.BlockSpec((1, D), lambda i: (0, 0))],
        out_specs=pl.BlockSpec((tm, D), lambda i: (i, 0)),
        out_shape=jax.ShapeDtypeStruct((M, D), BF16),
        compiler_params=_params("parallel"),
        name="rmsnorm",
    )(x, g.reshape(1, D))


def _matmul_kernel(a_ref, w_ref, o_ref):
    o_ref[...] = jnp.dot(a_ref[...], w_ref[...], preferred_element_type=F32).astype(o_ref.dtype)


def matmul(a, w, out_dtype, tm=1024, tn=1024):
    M, K = a.shape
    N = w.shape[1]
    tm, tn = min(tm, M), min(tn, N)
    return pl.pallas_call(
        _matmul_kernel,
        grid=(M // tm, N // tn),
        in_specs=[pl.BlockSpec((tm, K), lambda i, j: (i, 0)), pl.BlockSpec((K, tn), lambda i, j: (0, j))],
        out_specs=pl.BlockSpec((tm, tn), lambda i, j: (i, j)),
        out_shape=jax.ShapeDtypeStruct((M, N), out_dtype),
        compiler_params=_params("parallel", "parallel"),
        name="in_proj",
    )(a, w)


def _outproj_kernel(a_ref, w_ref, x_ref, g_ref, *out_refs, last):
    x_new = x_ref[...] + jnp.dot(a_ref[...], w_ref[...], preferred_element_type=F32)
    y = x_new * lax.rsqrt(jnp.mean(x_new * x_new, axis=-1, keepdims=True) + RMS_EPS) * g_ref[...]
    if last:
        out_refs[0][...] = y
    else:
        out_refs[0][...] = x_new
        out_refs[1][...] = y.astype(BF16)


def outproj(a, w, x, g, last, tm=256):
    M, K = a.shape
    D = w.shape[1]
    tm = min(tm, M)
    row = lambda i: (i, 0)
    if last:
        out_shape = jax.ShapeDtypeStruct((M, D), F32)
        out_specs = pl.BlockSpec((tm, D), row)
    else:
        out_shape = (jax.ShapeDtypeStruct((M, D), F32), jax.ShapeDtypeStruct((M, D), BF16))
        out_specs = (pl.BlockSpec((tm, D), row), pl.BlockSpec((tm, D), row))
    return pl.pallas_call(
        functools.partial(_outproj_kernel, last=last),
        grid=(M // tm,),
        in_specs=[
            pl.BlockSpec((tm, K), row),
            pl.BlockSpec((K, D), lambda i: (0, 0), pipeline_mode=pl.Buffered(1)),
            pl.BlockSpec((tm, D), row),
            pl.BlockSpec((1, D), lambda i: (0, 0)),
        ],
        out_specs=out_specs,
        out_shape=out_shape,
        compiler_params=_params("parallel"),
        name="out_proj",
    )(a, w, x, g.reshape(1, D))


def _attn_kernel(slopes_ref, lam_ref, g_ref, q_ref, k_ref, v_ref, gate_ref, o_ref,
                 m1_ref, l1_ref, a1_ref, m2_ref, l2_ref, a2_ref, *, tq, tk, nk, lambda_init):
    h = pl.program_id(1)
    i = pl.program_id(2)
    slope = slopes_ref[h]
    scale = ATTN_HEAD_DIM ** -0.5
    q = q_ref[...]
    q1 = q[:, :ATTN_HEAD_DIM]
    q2 = q[:, ATTN_HEAD_DIM:]
    for m_ref, l_ref, a_ref in ((m1_ref, l1_ref, a1_ref), (m2_ref, l2_ref, a2_ref)):
        m_ref[...] = jnp.full(m_ref.shape, NEG_BIG, F32)
        l_ref[...] = jnp.zeros(l_ref.shape, F32)
        a_ref[...] = jnp.zeros(a_ref.shape, F32)
    pos_q = (i * tq + lax.broadcasted_iota(jnp.int32, (tq, 1), 0)).astype(F32)

    def body(c, carry):
        start = pl.multiple_of(c * tk, tk)
        kc = k_ref[pl.ds(start, tk), :]
        vc = v_ref[pl.ds(start, tk), :]
        pos_k = (start + lax.broadcasted_iota(jnp.int32, (1, tk), 1)).astype(F32)
        bias = -slope * jnp.abs(pos_q - pos_k)
        maps = ((q1, kc[:, :ATTN_HEAD_DIM], m1_ref, l1_ref, a1_ref),
                (q2, kc[:, ATTN_HEAD_DIM:], m2_ref, l2_ref, a2_ref))
        for qq, kk, m_ref, l_ref, a_ref in maps:
            s = lax.dot_general(qq, kk, NT_DIMS, preferred_element_type=F32) * scale + bias
            m_old = m_ref[...]
            m_new = jnp.maximum(m_old, jnp.max(s, axis=-1, keepdims=True))
            p = jnp.exp(s - m_new)
            alpha = jnp.exp(m_old - m_new)
            l_ref[...] = alpha * l_ref[...] + jnp.sum(p, axis=-1, keepdims=True)
            a_ref[...] = alpha * a_ref[...] + jnp.dot(p.astype(BF16), vc, preferred_element_type=F32)
            m_ref[...] = m_new
        return carry

    lax.fori_loop(0, nk, body, 0)

    lv = lam_ref[...]
    lam = (jnp.exp(jnp.sum(lv[0:1] * lv[1:2], axis=-1, keepdims=True))
           - jnp.exp(jnp.sum(lv[2:3] * lv[3:4], axis=-1, keepdims=True)) + lambda_init)
    o = a1_ref[...] * (1.0 / l1_ref[...]) - lam * (a2_ref[...] * (1.0 / l2_ref[...]))
    o = o * lax.rsqrt(jnp.mean(o * o, axis=-1, keepdims=True) + RMS_EPS) * g_ref[...] * (1.0 - lambda_init)
    gate = gate_ref[...].astype(F32)
    o_ref[...] = (o * (gate * _sigmoid(gate))).astype(o_ref.dtype)


def diff_attention(proj, lam_vec, subln_g, B, T, lambda_init, tq=256, tk=512):
    tq, tk = min(tq, T), min(tk, T)
    nq, nk = T // tq, T // tk
    H = ATTN_HEADS
    slopes = jnp.exp2(-8.0 * jnp.arange(1, H + 1, dtype=F32) / H)
    kv_spec = lambda off: pl.BlockSpec((T, ATTN_V_DIM), lambda b, h, i: (b, off + h), pipeline_mode=pl.Buffered(1))
    return pl.pallas_call(
        functools.partial(_attn_kernel, tq=tq, tk=tk, nk=nk, lambda_init=lambda_init),
        grid=(B, H, nq),
        in_specs=[
            pl.BlockSpec(memory_space=pltpu.SMEM),
            pl.BlockSpec((4, ATTN_HEAD_DIM), lambda b, h, i: (0, 0)),
            pl.BlockSpec((1, ATTN_V_DIM), lambda b, h, i: (0, 0)),
            pl.BlockSpec((tq, ATTN_V_DIM), lambda b, h, i: (b * nq + i, h)),
            kv_spec(H),
            kv_spec(2 * H),
            pl.BlockSpec((tq, ATTN_V_DIM), lambda b, h, i: (b * nq + i, 3 * H + h)),
        ],
        out_specs=pl.BlockSpec((tq, ATTN_V_DIM), lambda b, h, i: (b * nq + i, h)),
        out_shape=jax.ShapeDtypeStruct((B * T, H * ATTN_V_DIM), BF16),
        scratch_shapes=[pltpu.VMEM((tq, 1), F32), pltpu.VMEM((tq, 1), F32), pltpu.VMEM((tq, ATTN_V_DIM), F32),
                        pltpu.VMEM((tq, 1), F32), pltpu.VMEM((tq, 1), F32), pltpu.VMEM((tq, ATTN_V_DIM), F32)],
        compiler_params=_params("parallel", "parallel", "arbitrary"),
        name="diff_attn",
    )(slopes, lam_vec.astype(F32), subln_g.reshape(1, ATTN_V_DIM).astype(F32), proj, proj, proj, proj)


def _dn_conv_kernel(x_ref, prev_ref, next_ref, w_ref, o_ref, ext_ref, *, tT, tc, nT, n_qk_blocks):
    i = pl.program_id(1)
    j = pl.program_id(2)
    halo = prev_ref.shape[0]
    prev = prev_ref[...].astype(F32)[halo - 8:, :]
    nxt = next_ref[...].astype(F32)[:8, :]
    ext_ref[0:8, :] = jnp.where(i > 0, prev, 0.0)
    ext_ref[8:8 + tT, :] = x_ref[...].astype(F32)
    ext_ref[8 + tT:16 + tT, :] = jnp.where(i < nT - 1, nxt, 0.0)
    w = w_ref[...]
    y = jnp.zeros((tT, tc), F32)
    for tap in range(CONV_W):
        y = y + w[tap:tap + 1, :] * ext_ref[pl.ds(8 + tap - CONV_W // 2, tT), :]
    y = y * _sigmoid(y)

    @pl.when(j < n_qk_blocks)
    def _():
        for g in range(tc // LANES):
            seg = y[:, g * LANES:(g + 1) * LANES]
            seg = seg * lax.rsqrt(jnp.sum(seg * seg, axis=-1, keepdims=True) + L2_EPS)
            o_ref[:, g * LANES:(g + 1) * LANES] = seg.astype(o_ref.dtype)

    @pl.when(j >= n_qk_blocks)
    def _():
        o_ref[...] = y.astype(o_ref.dtype)


def dn_conv(proj, conv_w, B, T, tT=512, tc=512):
    tT = min(tT, T)
    nT = T // tT
    halo = 16
    hb = tT // halo
    n_halo_blocks = B * T // halo
    return pl.pallas_call(
        functools.partial(_dn_conv_kernel, tT=tT, tc=tc, nT=nT, n_qk_blocks=2 * DN_KW // tc),
        grid=(B, nT, DN_CONV_DIM // tc),
        in_specs=[
            pl.BlockSpec((tT, tc), lambda b, i, j: (b * nT + i, j)),
            pl.BlockSpec((halo, tc), lambda b, i, j: (jnp.maximum((b * nT + i) * hb - 1, 0), j)),
            pl.BlockSpec((halo, tc), lambda b, i, j: (jnp.minimum((b * nT + i + 1) * hb, n_halo_blocks - 1), j)),
            pl.BlockSpec((CONV_W, tc), lambda b, i, j: (0, j)),
        ],
        out_specs=pl.BlockSpec((tT, tc), lambda b, i, j: (b * nT + i, j)),
        out_shape=jax.ShapeDtypeStruct((B * T, DN_CONV_DIM), BF16),
        scratch_shapes=[pltpu.VMEM((tT + 16, tc), F32)],
        compiler_params=_params("parallel", "parallel", "parallel"),
        name="dn_conv",
    )(proj, proj, proj, conv_w.astype(F32))


def _split3(x):
    hi = x.astype(BF16)
    r = x - hi.astype(F32)
    mid = r.astype(BF16)
    lo = (r - mid.astype(F32)).astype(BF16)
    return hi, mid, lo


def _dn_gates_kernel(ab_ref, alog_ref, dtb_ref, o_ref, ot_ref, *, tT):
    ab = ab_ref[...]
    x = ab + dtb_ref[...]
    softplus = jnp.maximum(x, 0.0) + jnp.log1p(jnp.exp(-jnp.abs(x)))
    g = -jnp.exp(alog_ref[...]) * softplus
    beta = _sigmoid(ab)
    r = lax.broadcasted_iota(jnp.int32, (tT, tT), 0)
    s = lax.broadcasted_iota(jnp.int32, (tT, tT), 1)
    same = (r // CHUNK) == (s // CHUNK)
    lower = jnp.where(same & (r >= s), 1.0, 0.0).astype(BF16)
    upper = jnp.where(same & (r <= s), 1.0, 0.0).astype(BF16)
    parts = _split3(g)
    cum_f = sum(jnp.dot(lower, p, preferred_element_type=F32) for p in parts)
    cum_b = sum(jnp.dot(upper, p, preferred_element_type=F32) for p in parts)
    lane = lax.broadcasted_iota(jnp.int32, (tT, LANES), 1)
    is_beta = (lane // DN_V_HEADS) % 2 == 1
    out = jnp.where(is_beta, beta, jnp.where(lane < 2 * DN_V_HEADS, cum_f, cum_b))
    o_ref[...] = out
    ot_ref[...] = out.T


def dn_gates(ab, a_log_f, dt_bias_f, a_log_b, dt_bias_b, B, T, tT=256):
    tT = min(tT, T)
    nT = T // tT
    zeros = jnp.zeros((DN_V_HEADS,), F32)
    alog = jnp.concatenate([a_log_f.astype(F32), zeros, a_log_b.astype(F32), zeros]).reshape(1, LANES)
    dtb = jnp.concatenate([dt_bias_f.astype(F32), zeros, dt_bias_b.astype(F32), zeros]).reshape(1, LANES)
    return pl.pallas_call(
        functools.partial(_dn_gates_kernel, tT=tT),
        grid=(B, nT),
        in_specs=[
            pl.BlockSpec((tT, LANES), lambda b, i: (b * nT + i, 0)),
            pl.BlockSpec((1, LANES), lambda b, i: (0, 0)),
            pl.BlockSpec((1, LANES), lambda b, i: (0, 0)),
        ],
        out_specs=(pl.BlockSpec((tT, LANES), lambda b, i: (b * nT + i, 0)),
                   pl.BlockSpec((LANES, tT), lambda b, i: (b, i))),
        out_shape=(jax.ShapeDtypeStruct((B * T, LANES), F32), jax.ShapeDtypeStruct((B * LANES, T), F32)),
        compiler_params=_params("parallel", "parallel"),
        name="dn_gates",
    )(ab, alog, dtb)


def _dn_chunk_kernel(q_ref, k_ref, v_ref, gt_ref, gtt_ref, o_ref, s_ref, *, nc, rev):
    kh = pl.program_id(1)
    t = pl.program_id(2)

    @pl.when(t == 0)
    def _():
        s_ref[...] = jnp.zeros(s_ref.shape, F32)

    scale = DN_DK ** -0.5
    r = lax.broadcasted_iota(jnp.int32, (CHUNK, CHUNK), 0)
    c = lax.broadcasted_iota(jnp.int32, (CHUNK, CHUNK), 1)
    incl = (r <= c) if rev else (r >= c)
    strict = (r < c) if rev else (r > c)
    lane = lax.broadcasted_iota(jnp.int32, (CHUNK, LANES), 1)
    g_off = 2 * DN_V_HEADS if rev else 0
    last_row = 0 if rev else CHUNK - 1

    g_rows = [gtt_ref[pl.ds(g_off + 2 * kh + hh, 1), :] for hh in range(2)]
    order = range(nc - 1, -1, -1) if rev else range(nc)
    for ci in order:
        rows = slice(ci * CHUNK, (ci + 1) * CHUNK)
        qc = q_ref[rows, :]
        kc = k_ref[rows, :]
        kf = kc.astype(F32)
        qf = qc.astype(F32)
        gram = lax.dot_general(jnp.concatenate([kc, qc], axis=0), kc, NT_DIMS, preferred_element_type=F32)
        kk = gram[:CHUNK]
        qk = gram[CHUNK:]
        gt = gt_ref[rows, :]
        for hh in range(2):
            hv = 2 * kh + hh
            gc = jnp.sum(jnp.where(lane == g_off + hv, gt, 0.0), axis=-1, keepdims=True)
            beta = jnp.sum(jnp.where(lane == g_off + DN_V_HEADS + hv, gt, 0.0), axis=-1, keepdims=True)
            g_row = g_rows[hh][:, rows]
            decay = jnp.exp(jnp.where(incl, gc - g_row, -jnp.inf))
            m = jnp.where(strict, -(kk * beta * decay), 0.0)
            a_intra = qk * scale * decay
            y = m
            for _ in range(5):
                mb = m.astype(BF16)
                m = jnp.dot(mb, mb, preferred_element_type=F32)
                y = y + m + jnp.dot(y.astype(BF16), m.astype(BF16), preferred_element_type=F32)
            eg = jnp.exp(gc)
            g_last = gc[last_row:last_row + 1, :]
            vb = v_ref[rows, hh * DN_DV:(hh + 1) * DN_DV].astype(F32) * beta
            kbg = kf * (beta * eg)
            rhs = jnp.concatenate([vb, kbg], axis=1)
            uw = rhs + jnp.dot(y.astype(BF16), rhs.astype(BF16), preferred_element_type=F32)
            u = uw[:, :DN_DV]
            w = uw[:, DN_DV:]
            qd = qf * (scale * eg)
            kd = kf * jnp.exp(g_last - gc)
            S = s_ref[hh]
            ws = jnp.dot(jnp.concatenate([w, qd], axis=0).astype(BF16), S.astype(BF16), preferred_element_type=F32)
            v_new = u - ws[:CHUNK]
            vnb = v_new.astype(BF16)
            o = ws[CHUNK:] + jnp.dot(a_intra.astype(BF16), vnb, preferred_element_type=F32)
            s_ref[hh] = S * jnp.exp(g_last) + lax.dot_general(kd.astype(BF16), vnb, TN_DIMS, preferred_element_type=F32)
            o_ref[rows, hh * DN_DV:(hh + 1) * DN_DV] = o


def dn_chunk(conv_out, gates, gates_t, B, T, rev, tb=256):
    tb = min(tb, T)
    nT = T // tb
    tmap = (lambda t: nT - 1 - t) if rev else (lambda t: t)
    return pl.pallas_call(
        functools.partial(_dn_chunk_kernel, nc=tb // CHUNK, rev=rev),
        grid=(B, DN_K_HEADS, nT),
        in_specs=[
            pl.BlockSpec((tb, DN_DK), lambda b, kh, t: (b * nT + tmap(t), kh)),
            pl.BlockSpec((tb, DN_DK), lambda b, kh, t: (b * nT + tmap(t), DN_K_HEADS + kh)),
            pl.BlockSpec((tb, 2 * DN_DV), lambda b, kh, t: (b * nT + tmap(t), DN_K_HEADS + kh)),
            pl.BlockSpec((tb, LANES), lambda b, kh, t: (b * nT + tmap(t), 0)),
            pl.BlockSpec((LANES, tb), lambda b, kh, t: (b, tmap(t))),
        ],
        out_specs=pl.BlockSpec((tb, 2 * DN_DV), lambda b, kh, t: (b * nT + tmap(t), kh)),
        out_shape=jax.ShapeDtypeStruct((B * T, DN_VW), F32),
        scratch_shapes=[pltpu.VMEM((2, DN_DK, DN_DV), F32)],
        compiler_params=_params("parallel", "parallel", "arbitrary"),
        name="dn_chunk_bwd" if rev else "dn_chunk_fwd",
    )(conv_out, conv_out, conv_out, gates, gates_t)


def _dn_gnorm_kernel(of_ref, ob_ref, z_ref, g_ref, o_ref, *, tc):
    g = g_ref[...]
    for s in range(tc // LANES):
        cols = slice(s * LANES, (s + 1) * LANES)
        o = of_ref[:, cols] + ob_ref[:, cols]
        o = o * lax.rsqrt(jnp.mean(o * o, axis=-1, keepdims=True) + RMS_EPS) * g
        z = z_ref[:, cols].astype(F32)
        o_ref[:, cols] = (o * (z * _sigmoid(z))).astype(o_ref.dtype)


def dn_gnorm(o_f, o_b, proj, norm_g, tm=512, tc=512):
    M = o_f.shape[0]
    tm = min(tm, M)
    z_off = DN_CONV_DIM // tc
    return pl.pallas_call(
        functools.partial(_dn_gnorm_kernel, tc=tc),
        grid=(M // tm, DN_VW // tc),
        in_specs=[
            pl.BlockSpec((tm, tc), lambda i, j: (i, j)),
            pl.BlockSpec((tm, tc), lambda i, j: (i, j)),
            pl.BlockSpec((tm, tc), lambda i, j: (i, z_off + j)),
            pl.BlockSpec((1, DN_DV), lambda i, j: (0, 0)),
        ],
        out_specs=pl.BlockSpec((tm, tc), lambda i, j: (i, j)),
        out_shape=jax.ShapeDtypeStruct((M, DN_VW), BF16),
        compiler_params=_params("parallel", "parallel"),
        name="dn_gnorm",
    )(o_f, o_b, proj, norm_g.reshape(1, DN_DV).astype(F32))


def _lambda_init(layer_idx):
    return 0.8 - 0.6 * math.exp(-0.3 * layer_idx)


def _trunk(x, p):
    B, T, D = x.shape
    xf = x.reshape(B * T, D)
    h = rmsnorm(xf, p["norm_g"][0])
    for i in range(DEPTH):
        j = i // 2
        if i % 2 == 0:
            proj = matmul(h, p["attn_w_in"][j], BF16)
            a = diff_attention(proj, p["attn_lambda"][j], p["attn_subln_g"][j], B, T, _lambda_init(i))
            w_out = p["attn_w_out"][j]
        else:
            proj = matmul(h, p["dn_w_main"][j], BF16)
            ab = matmul(h, p["dn_w_ab"][j], F32)
            conv_out = dn_conv(proj, p["dn_conv_w"][j], B, T)
            gates, gates_t = dn_gates(ab, p["dn_a_log_fwd"][j], p["dn_dt_bias_fwd"][j],
                                      p["dn_a_log_bwd"][j], p["dn_dt_bias_bwd"][j], B, T)
            o_f = dn_chunk(conv_out, gates, gates_t, B, T, rev=False)
            o_b = dn_chunk(conv_out, gates, gates_t, B, T, rev=True)
            a = dn_gnorm(o_f, o_b, proj, p["dn_norm_g"][j])
            w_out = p["dn_w_out"][j]
        if i + 1 < DEPTH:
            xf, h = outproj(a, w_out, xf, p["norm_g"][i + 1], last=False)
        else:
            xf = outproj(a, w_out, xf, p["final_norm_g"], last=True)
    return xf.reshape(B, T, D)


def kernel(x_prompt, x_sample, norm_g, attn_w_in, attn_lambda, attn_subln_g, attn_w_out, dn_w_in, dn_conv_w, dn_a_log_fwd, dn_dt_bias_fwd, dn_a_log_bwd, dn_dt_bias_bwd, dn_norm_g, dn_w_out, final_norm_g):
    p = {
        "norm_g": norm_g.astype(F32),
        "attn_w_in": attn_w_in.astype(BF16),
        "attn_lambda": attn_lambda,
        "attn_subln_g": attn_subln_g,
        "attn_w_out": attn_w_out.astype(BF16),
        "dn_w_main": dn_w_in[:, :, :DN_MAIN].astype(BF16),
        "dn_w_ab": dn_w_in[:, :, DN_MAIN:].astype(BF16),
        "dn_conv_w": dn_conv_w,
        "dn_a_log_fwd": dn_a_log_fwd,
        "dn_dt_bias_fwd": dn_dt_bias_fwd,
        "dn_a_log_bwd": dn_a_log_bwd,
        "dn_dt_bias_bwd": dn_dt_bias_bwd,
        "dn_norm_g": dn_norm_g,
        "dn_w_out": dn_w_out.astype(BF16),
        "final_norm_g": final_norm_g.astype(F32),
    }
    return (_trunk(x_prompt, p), _trunk(x_sample, p))
```

```python
import functools
import math

import jax
import jax.numpy as jnp
from jax import lax
from jax.experimental import pallas as pl
from jax.experimental.pallas import tpu as pltpu

F32 = jnp.float32
BF16 = jnp.bfloat16

D_MODEL = 2048
DEPTH = 4
RMS_EPS = 1e-6
L2_EPS = 1e-6

ATTN_HEADS = 8
ATTN_HEAD_DIM = 128
ATTN_V_DIM = 256
ATTN_IN = 8192

DN_K_HEADS = 16
DN_V_HEADS = 32
DN_DK = 128
DN_DV = 128
DN_KW = 2048
DN_VW = 4096
DN_CONV_DIM = 8192
DN_MAIN = DN_CONV_DIM + DN_VW
CONV_W = 5
CHUNK = 64

LANES = 128
VMEM_LIMIT = 56 * 1024 * 1024
NEG_BIG = -1e30

NT_DIMS = (((1,), (1,)), ((), ()))
TN_DIMS = (((0,), (0,)), ((), ()))


def _params(*sem):
    return pltpu.CompilerParams(dimension_semantics=sem, vmem_limit_bytes=VMEM_LIMIT)


def _sigmoid(x):
    return 1.0 / (1.0 + jnp.exp(-x))


def _rmsnorm_kernel(x_ref, g_ref, o_ref):
    x = x_ref[...]
    y = x * lax.rsqrt(jnp.mean(x * x, axis=-1, keepdims=True) + RMS_EPS)
    o_ref[...] = (y * g_ref[...]).astype(o_ref.dtype)


def rmsnorm(x, g, tm=512):
    M, D = x.shape
    tm = min(tm, M)
    return pl.pallas_call(
        _rmsnorm_kernel,
        grid=(M // tm,),
        in_specs=[pl.BlockSpec((tm, D), lambda i: (i, 0)), pl.BlockSpec((1, D), lambda i: (0, 0))],
        out_specs=pl.BlockSpec((tm, D), lambda i: (i, 0)),
        out_shape=jax.ShapeDtypeStruct((M, D), BF16),
        compiler_params=_params("parallel"),
        name="rmsnorm",
    )(x, g.reshape(1, D))


def _matmul_kernel(a_ref, w_ref, o_ref):
    o_ref[...] = jnp.dot(a_ref[...], w_ref[...], preferred_element_type=F32).astype(o_ref.dtype)


def matmul(a, w, out_dtype, tm=1024, tn=1024):
    M, K = a.shape
    N = w.shape[1]
    tm, tn = min(tm, M), min(tn, N)
    return pl.pallas_call(
        _matmul_kernel,
        grid=(M // tm, N // tn),
        in_specs=[pl.BlockSpec((tm, K), lambda i, j: (i, 0)), pl.BlockSpec((K, tn), lambda i, j: (0, j))],
        out_specs=pl.BlockSpec((tm, tn), lambda i, j: (i, j)),
        out_shape=jax.ShapeDtypeStruct((M, N), out_dtype),
        compiler_params=_params("parallel", "parallel"),
        name="in_proj",
    )(a, w)


def _outproj_kernel(a_ref, w_ref, x_ref, g_ref, *out_refs, last):
    x_new = x_ref[...] + jnp.dot(a_ref[...], w_ref[...], preferred_element_type=F32)
    y = x_new * lax.rsqrt(jnp.mean(x_new * x_new, axis=-1, keepdims=True) + RMS_EPS) * g_ref[...]
    if last:
        out_refs[0][...] = y
    else:
        out_refs[0][...] = x_new
        out_refs[1][...] = y.astype(BF16)


def outproj(a, w, x, g, last, tm=256):
    M, K = a.shape
    D = w.shape[1]
    tm = min(tm, M)
    row = lambda i: (i, 0)
    if last:
        out_shape = jax.ShapeDtypeStruct((M, D), F32)
        out_specs = pl.BlockSpec((tm, D), row)
    else:
        out_shape = (jax.ShapeDtypeStruct((M, D), F32), jax.ShapeDtypeStruct((M, D), BF16))
        out_specs = (pl.BlockSpec((tm, D), row), pl.BlockSpec((tm, D), row))
    return pl.pallas_call(
        functools.partial(_outproj_kernel, last=last),
        grid=(M // tm,),
        in_specs=[
            pl.BlockSpec((tm, K), row),
            pl.BlockSpec((K, D), lambda i: (0, 0), pipeline_mode=pl.Buffered(1)),
            pl.BlockSpec((tm, D), row),
            pl.BlockSpec((1, D), lambda i: (0, 0)),
        ],
        out_specs=out_specs,
        out_shape=out_shape,
        compiler_params=_params("parallel"),
        name="out_proj",
    )(a, w, x, g.reshape(1, D))


ATTN_TQ = 256
ATTN_SUB = 256
LOG2E = 1.4426950408889634


def _split3(x):
    hi = x.astype(BF16)
    r = x - hi.astype(F32)
    mid = r.astype(BF16)
    lo = (r - mid.astype(F32)).astype(BF16)
    return hi, mid, lo


def _split3_f32(x):
    hi, mid, lo = _split3(x)
    return hi.astype(F32), mid.astype(F32), lo.astype(F32)


def _attn_prep_kernel(slopes_ref, k_ref, v_ref, kaug_ref, vt_ref, *, tp):
    h = pl.program_id(1)
    i = pl.program_id(2)
    shape = (tp, LANES)
    lane = lax.broadcasted_iota(jnp.int32, shape, 1)
    pos = lax.broadcasted_iota(jnp.int32, shape, 0) + i * tp
    hi, mid, lo = _split3_f32(jnp.full(shape, slopes_ref[h] * LOG2E, F32))
    j_lo = (pos % ATTN_SUB).astype(F32)
    j_hi = (((pos // ATTN_SUB) % 2) * ATTN_SUB).astype(F32)
    ext = jnp.where(lane == 0, -hi, jnp.where(lane == 1, -mid, jnp.where(lane == 2, -lo,
          jnp.where(lane < 6, j_lo, jnp.where(lane < 9, j_hi, 0.0))))).astype(BF16)
    k = k_ref[...]
    kaug_ref[:, 0 * LANES:1 * LANES] = k[:, :ATTN_HEAD_DIM]
    kaug_ref[:, 1 * LANES:2 * LANES] = ext
    kaug_ref[:, 2 * LANES:3 * LANES] = k[:, ATTN_HEAD_DIM:]
    kaug_ref[:, 3 * LANES:4 * LANES] = ext
    vt_ref[0] = v_ref[...].astype(F32).T.astype(BF16)


def _attn_kernel(slopes_ref, lam_ref, g_ref, q_ref, kaug_ref, vt_ref, gate_ref, o_ref,
                 qv_ref, bias_ref, s_ref, p_ref, acc_ref, *, tq, tk, nk, lambda_init):
    h = pl.program_id(1)
    i = pl.program_id(2)
    c_slope = slopes_ref[h] * LOG2E
    i0 = i * tq
    q = q_ref[...].astype(F32) * (ATTN_HEAD_DIM ** -0.5 * LOG2E)
    q1 = q[:, :ATTN_HEAD_DIM].astype(BF16)
    q2 = q[:, ATTN_HEAD_DIM:].astype(BF16)
    shape = (tq, LANES)
    lane = lax.broadcasted_iota(jnp.int32, shape, 1)
    row = lax.broadcasted_iota(jnp.int32, shape, 0).astype(F32)
    hi, mid, lo = _split3_f32(jnp.full(shape, c_slope, F32))
    c_part = jnp.where(lane % 3 == 0, hi, jnp.where(lane % 3 == 1, mid, lo))
    ext = jnp.where(lane < 3, row, jnp.where(lane < 9, c_part, 0.0))
    for mode, e in ((0, ext), (1, jnp.zeros(shape, F32)), (2, -ext)):
        eb = e.astype(BF16)
        qv_ref[mode, 0] = jnp.concatenate([q1, eb], axis=1)
        qv_ref[mode, 1] = jnp.concatenate([q2, eb], axis=1)
    c_diag = i0 // tk
    pos_k = c_diag * tk + lax.broadcasted_iota(jnp.int32, (tk, 1), 0)
    pos_q = i0 + lax.broadcasted_iota(jnp.int32, (1, tq), 1)
    bias_ref[0] = jnp.zeros((tk, tq), F32)
    bias_ref[1] = -c_slope * jnp.abs(pos_k - pos_q).astype(F32)
    acc_ref[...] = jnp.zeros(acc_ref.shape, F32)

    def mode_of(c):
        return (c >= c_diag).astype(jnp.int32) + (c > c_diag).astype(jnp.int32)

    def stage_scores(c, par):
        kc = kaug_ref[pl.ds(pl.multiple_of(c * tk, tk), tk), :]
        mode = mode_of(c)
        for mi in range(2):
            kk = kc[:, 2 * mi * LANES:(2 * mi + 2) * LANES]
            s_ref[par, mi] = lax.dot_general(kk, qv_ref[mode, mi], NT_DIMS, preferred_element_type=F32)

    def stage_softmax(c, par, carry):
        is_diag = c == c_diag
        dist = jnp.maximum(i0 - c * tk, c * tk - i0).astype(F32)
        const = jnp.where(is_diag, 0.0, -c_slope * dist)
        bias = bias_ref[is_diag.astype(jnp.int32)]
        out = []
        for mi in range(2):
            m, l = carry[3 * mi], carry[3 * mi + 1]
            s = s_ref[par, mi] + bias
            m_new = jnp.maximum(m, jnp.max(s, axis=0, keepdims=True) + const)
            p = jnp.exp2(s - (m_new - const))
            alpha = jnp.exp2(m - m_new)
            l = alpha * l + jnp.sum(p, axis=0, keepdims=True)
            p_ref[par, mi] = p.astype(BF16)
            out += [m_new, l, alpha]
        return tuple(out)

    def stage_pv(c, par, carry):
        vtc = vt_ref[c]
        for mi in range(2):
            alpha = carry[3 * mi + 2]
            acc_ref[mi] = alpha * acc_ref[mi] + jnp.dot(vtc, p_ref[par, mi], preferred_element_type=F32)

    def tick(t, par, carry, first=False, last=False):
        if not first or (isinstance(t, int) and t >= 2):
            stage_pv(t - 2, par, carry)
        if (not first or t >= 1) and (not last or t - 1 < nk):
            carry = stage_softmax(t - 1, 1 - par, carry)
        if not last or t < nk:
            stage_scores(t, par)
        return carry

    neg = jnp.full((1, tq), NEG_BIG, F32)
    zero = jnp.zeros((1, tq), F32)
    carry = (neg, zero, zero, neg, zero, zero)
    n_pro = min(2, nk)
    for t in range(n_pro):
        carry = tick(t, t % 2, carry, first=True)
    n_pairs = max(nk - 2, 0) // 2

    def pair(u, cr):
        t = 2 + 2 * u
        cr = tick(t, 0, cr)
        return tick(t + 1, 1, cr)

    carry = lax.fori_loop(0, n_pairs, pair, carry)
    for t in range(n_pro + 2 * n_pairs, nk + 2):
        carry = tick(t, t % 2, carry, first=t < 2, last=True)
    l1, l2 = carry[1], carry[4]

    lv = lam_ref[...]
    lam = (jnp.exp(jnp.sum(lv[0:1] * lv[1:2], axis=-1, keepdims=True))
           - jnp.exp(jnp.sum(lv[2:3] * lv[3:4], axis=-1, keepdims=True)) + lambda_init)
    o = (acc_ref[0] * (1.0 / l1) - lam * (acc_ref[1] * (1.0 / l2))).T
    o = o * lax.rsqrt(jnp.mean(o * o, axis=-1, keepdims=True) + RMS_EPS) * g_ref[...] * (1.0 - lambda_init)
    gate = gate_ref[...].astype(F32)
    o_ref[...] = (o * (gate * _sigmoid(gate))).astype(o_ref.dtype)


def diff_attention(proj, lam_vec, subln_g, B, T, lambda_init, tk=512):
    tq, tk = min(ATTN_TQ, T), min(tk, T)
    nq, nk = T // tq, T // tk
    H = ATTN_HEADS
    slopes = jnp.exp2(-8.0 * jnp.arange(1, H + 1, dtype=F32) / H)
    smem = pl.BlockSpec(memory_space=pltpu.SMEM)
    kaug, vt = pl.pallas_call(
        functools.partial(_attn_prep_kernel, tp=tk),
        grid=(B, H, nk),
        in_specs=[
            smem,
            pl.BlockSpec((tk, ATTN_V_DIM), lambda b, h, i: (b * nk + i, H + h)),
            pl.BlockSpec((tk, ATTN_V_DIM), lambda b, h, i: (b * nk + i, 2 * H + h)),
        ],
        out_specs=(pl.BlockSpec((tk, 4 * LANES), lambda b, h, i: (b * nk + i, h)),
                   pl.BlockSpec((1, ATTN_V_DIM, tk), lambda b, h, i: ((b * H + h) * nk + i, 0, 0))),
        out_shape=(jax.ShapeDtypeStruct((B * T, H * 4 * LANES), BF16),
                   jax.ShapeDtypeStruct((B * H * nk, ATTN_V_DIM, tk), BF16)),
        compiler_params=_params("parallel", "parallel", "parallel"),
        name="attn_prep",
    )(slopes, proj, proj)
    return pl.pallas_call(
        functools.partial(_attn_kernel, tq=tq, tk=tk, nk=nk, lambda_init=lambda_init),
        grid=(B, H, nq),
        in_specs=[
            smem,
            pl.BlockSpec((4, ATTN_HEAD_DIM), lambda b, h, i: (0, 0)),
            pl.BlockSpec((1, ATTN_V_DIM), lambda b, h, i: (0, 0)),
            pl.BlockSpec((tq, ATTN_V_DIM), lambda b, h, i: (b * nq + i, h)),
            pl.BlockSpec((T, 4 * LANES), lambda b, h, i: (b, h), pipeline_mode=pl.Buffered(1)),
            pl.BlockSpec((nk, ATTN_V_DIM, tk), lambda b, h, i: (b * H + h, 0, 0), pipeline_mode=pl.Buffered(1)),
            pl.BlockSpec((tq, ATTN_V_DIM), lambda b, h, i: (b * nq + i, 3 * H + h)),
        ],
        out_specs=pl.BlockSpec((tq, ATTN_V_DIM), lambda b, h, i: (b * nq + i, h)),
        out_shape=jax.ShapeDtypeStruct((B * T, H * ATTN_V_DIM), BF16),
        scratch_shapes=[pltpu.VMEM((3, 2, tq, 2 * LANES), BF16), pltpu.VMEM((2, tk, tq), F32),
                        pltpu.VMEM((2, 2, tk, tq), F32), pltpu.VMEM((2, 2, tk, tq), BF16),
                        pltpu.VMEM((2, ATTN_V_DIM, tq), F32)],
        compiler_params=_params("parallel", "parallel", "arbitrary"),
        name="diff_attn",
    )(slopes, lam_vec.astype(F32), subln_g.reshape(1, ATTN_V_DIM).astype(F32), proj, kaug, vt, proj)


def _dn_conv_kernel(x_ref, prev_ref, next_ref, w_ref, o_ref, ext_ref, *, tT, tc, nT, n_qk_blocks):
    i = pl.program_id(1)
    j = pl.program_id(2)
    halo = prev_ref.shape[0]
    prev = prev_ref[...].astype(F32)[halo - 8:, :]
    nxt = next_ref[...].astype(F32)[:8, :]
    ext_ref[0:8, :] = jnp.where(i > 0, prev, 0.0)
    ext_ref[8:8 + tT, :] = x_ref[...].astype(F32)
    ext_ref[8 + tT:16 + tT, :] = jnp.where(i < nT - 1, nxt, 0.0)
    w = w_ref[...]
    y = jnp.zeros((tT, tc), F32)
    for tap in range(CONV_W):
        y = y + w[tap:tap + 1, :] * ext_ref[pl.ds(8 + tap - CONV_W // 2, tT), :]
    y = y * _sigmoid(y)

    @pl.when(j < n_qk_blocks)
    def _():
        for g in range(tc // LANES):
            seg = y[:, g * LANES:(g + 1) * LANES]
            seg = seg * lax.rsqrt(jnp.sum(seg * seg, axis=-1, keepdims=True) + L2_EPS)
            o_ref[:, g * LANES:(g + 1) * LANES] = seg.astype(o_ref.dtype)

    @pl.when(j >= n_qk_blocks)
    def _():
        o_ref[...] = y.astype(o_ref.dtype)


def dn_conv(proj, conv_w, B, T, tT=512, tc=512):
    tT = min(tT, T)
    nT = T // tT
    halo = 16
    hb = tT // halo
    n_halo_blocks = B * T // halo
    return pl.pallas_call(
        functools.partial(_dn_conv_kernel, tT=tT, tc=tc, nT=nT, n_qk_blocks=2 * DN_KW // tc),
        grid=(B, nT, DN_CONV_DIM // tc),
        in_specs=[
            pl.BlockSpec((tT, tc), lambda b, i, j: (b * nT + i, j)),
            pl.BlockSpec((halo, tc), lambda b, i, j: (jnp.maximum((b * nT + i) * hb - 1, 0), j)),
            pl.BlockSpec((halo, tc), lambda b, i, j: (jnp.minimum((b * nT + i + 1) * hb, n_halo_blocks - 1), j)),
            pl.BlockSpec((CONV_W, tc), lambda b, i, j: (0, j)),
        ],
        out_specs=pl.BlockSpec((tT, tc), lambda b, i, j: (b * nT + i, j)),
        out_shape=jax.ShapeDtypeStruct((B * T, DN_CONV_DIM), BF16),
        scratch_shapes=[pltpu.VMEM((tT + 16, tc), F32)],
        compiler_params=_params("parallel", "parallel", "parallel"),
        name="dn_conv",
    )(proj, proj, proj, conv_w.astype(F32))


def _dn_gates_kernel(ab_ref, alog_ref, dtb_ref, o_ref, ot_ref, *, tT):
    ab = ab_ref[...]
    x = ab + dtb_ref[...]
    softplus = jnp.maximum(x, 0.0) + jnp.log1p(jnp.exp(-jnp.abs(x)))
    g = -jnp.exp(alog_ref[...]) * softplus
    beta = _sigmoid(ab)
    r = lax.broadcasted_iota(jnp.int32, (tT, tT), 0)
    s = lax.broadcasted_iota(jnp.int32, (tT, tT), 1)
    same = (r // CHUNK) == (s // CHUNK)
    lower = jnp.where(same & (r >= s), 1.0, 0.0).astype(BF16)
    upper = jnp.where(same & (r <= s), 1.0, 0.0).astype(BF16)
    parts = _split3(g)
    cum_f = sum(jnp.dot(lower, p, preferred_element_type=F32) for p in parts)
    cum_b = sum(jnp.dot(upper, p, preferred_element_type=F32) for p in parts)
    lane = lax.broadcasted_iota(jnp.int32, (tT, LANES), 1)
    is_beta = (lane // DN_V_HEADS) % 2 == 1
    out = jnp.where(is_beta, beta, jnp.where(lane < 2 * DN_V_HEADS, cum_f, cum_b))
    o_ref[...] = out
    ot_ref[...] = out.T


def dn_gates(ab, a_log_f, dt_bias_f, a_log_b, dt_bias_b, B, T, tT=256):
    tT = min(tT, T)
    nT = T // tT
    zeros = jnp.zeros((DN_V_HEADS,), F32)
    alog = jnp.concatenate([a_log_f.astype(F32), zeros, a_log_b.astype(F32), zeros]).reshape(1, LANES)
    dtb = jnp.concatenate([dt_bias_f.astype(F32), zeros, dt_bias_b.astype(F32), zeros]).reshape(1, LANES)
    return pl.pallas_call(
        functools.partial(_dn_gates_kernel, tT=tT),
        grid=(B, nT),
        in_specs=[
            pl.BlockSpec((tT, LANES), lambda b, i: (b * nT + i, 0)),
            pl.BlockSpec((1, LANES), lambda b, i: (0, 0)),
            pl.BlockSpec((1, LANES), lambda b, i: (0, 0)),
        ],
        out_specs=(pl.BlockSpec((tT, LANES), lambda b, i: (b * nT + i, 0)),
                   pl.BlockSpec((LANES, tT), lambda b, i: (b, i))),
        out_shape=(jax.ShapeDtypeStruct((B * T, LANES), F32), jax.ShapeDtypeStruct((B * LANES, T), F32)),
        compiler_params=_params("parallel", "parallel"),
        name="dn_gates",
    )(ab, alog, dtb)


DN_GROUP = 2


def _dn_chunk_kernel(q_ref, k_ref, v_ref, gt_ref, gtt_ref, o_ref, s_ref, *, nc, rev):
    kh0 = pl.program_id(1) * DN_GROUP
    t = pl.program_id(2)

    @pl.when(t == 0)
    def _():
        s_ref[...] = jnp.zeros(s_ref.shape, F32)

    scale = DN_DK ** -0.5
    r = lax.broadcasted_iota(jnp.int32, (CHUNK, CHUNK), 0)
    c = lax.broadcasted_iota(jnp.int32, (CHUNK, CHUNK), 1)
    incl = (r <= c) if rev else (r >= c)
    strict = (r < c) if rev else (r > c)
    lane = lax.broadcasted_iota(jnp.int32, (CHUNK, LANES), 1)
    g_off = 2 * DN_V_HEADS if rev else 0
    last_row = 0 if rev else CHUNK - 1
    dot = functools.partial(jnp.dot, preferred_element_type=F32)

    heads = [(g, hh) for g in range(DN_GROUP) for hh in range(2)]
    g_rows = {(g, hh): gtt_ref[pl.ds(g_off + 2 * (kh0 + g) + hh, 1), :] for g, hh in heads}
    order = list(range(nc - 1, -1, -1) if rev else range(nc))

    inst = []
    for ci in order:
        rows = slice(ci * CHUNK, (ci + 1) * CHUNK)
        gt = gt_ref[rows, :]
        for g in range(DN_GROUP):
            qc = q_ref[rows, g * DN_DK:(g + 1) * DN_DK]
            kc = k_ref[rows, g * DN_DK:(g + 1) * DN_DK]
            kf = kc.astype(F32)
            qf = qc.astype(F32)
            gram = lax.dot_general(jnp.concatenate([kc, qc], axis=0), kc, NT_DIMS, preferred_element_type=F32)
            kk = gram[:CHUNK]
            qk = gram[CHUNK:]
            for hh in range(2):
                hv = 2 * (kh0 + g) + hh
                gc = jnp.sum(jnp.where(lane == g_off + hv, gt, 0.0), axis=-1, keepdims=True)
                beta = jnp.sum(jnp.where(lane == g_off + DN_V_HEADS + hv, gt, 0.0), axis=-1, keepdims=True)
                g_row = g_rows[(g, hh)][:, rows]
                decay = jnp.exp(jnp.where(incl, gc - g_row, -jnp.inf))
                eg = jnp.exp(gc)
                g_last = gc[last_row:last_row + 1, :]
                col = (2 * g + hh) * DN_DV
                vb = v_ref[rows, col:col + DN_DV].astype(F32) * beta
                inst.append(dict(
                    rows=rows, col=col, chain=2 * g + hh,
                    m=jnp.where(strict, -(kk * beta * decay), 0.0),
                    a=(qk * scale * decay).astype(BF16),
                    rhs=jnp.concatenate([vb, kf * (beta * eg)], axis=1),
                    kd=(kf * jnp.exp(g_last - gc)).astype(BF16),
                    qd=qf * (scale * eg),
                    egl=jnp.exp(g_last)))

    ms = [d["m"] for d in inst]
    ys = list(ms)
    for _ in range(5):
        mbs = [m.astype(BF16) for m in ms]
        ms = [dot(mb, mb) for mb in mbs]
        ys = [y + m + dot(y.astype(BF16), m.astype(BF16)) for y, m in zip(ys, ms)]

    uws = [(d["rhs"] + dot(y.astype(BF16), d["rhs"].astype(BF16))).astype(BF16) for d, y in zip(inst, ys)]
    kd_uws = [lax.dot_general(d["kd"], uw, TN_DIMS, preferred_element_type=F32) for d, uw in zip(inst, uws)]
    a_uws = [dot(d["a"], uw) for d, uw in zip(inst, uws)]
    for d, kd_uw, a_uw in zip(inst, kd_uws, a_uws):
        d["lhs"] = jnp.concatenate([kd_uw[:, DN_DV:], d["qd"] - a_uw[:, DN_DV:]], axis=0).astype(BF16)
        d["r"] = kd_uw[:, :DN_DV]
        d["o"] = a_uw[:, :DN_DV]

    states = [s_ref[j] for j in range(2 * DN_GROUP)]
    for d in inst:
        S = states[d["chain"]]
        res = dot(d["lhs"], S.astype(BF16))
        states[d["chain"]] = d["egl"] * S - res[:DN_DK] + d["r"]
        o_ref[d["rows"], d["col"]:d["col"] + DN_DV] = res[DN_DK:] + d["o"]
    for j in range(2 * DN_GROUP):
        s_ref[j] = states[j]


def dn_chunk(conv_out, gates, gates_t, B, T, rev, tb=256):
    tb = min(tb, T)
    nT = T // tb
    G = DN_GROUP
    n_groups = DN_K_HEADS // G
    tmap = (lambda t: nT - 1 - t) if rev else (lambda t: t)
    return pl.pallas_call(
        functools.partial(_dn_chunk_kernel, nc=tb // CHUNK, rev=rev),
        grid=(B, n_groups, nT),
        in_specs=[
            pl.BlockSpec((tb, G * DN_DK), lambda b, kh, t: (b * nT + tmap(t), kh)),
            pl.BlockSpec((tb, G * DN_DK), lambda b, kh, t: (b * nT + tmap(t), n_groups + kh)),
            pl.BlockSpec((tb, 2 * G * DN_DV), lambda b, kh, t: (b * nT + tmap(t), n_groups + kh)),
            pl.BlockSpec((tb, LANES), lambda b, kh, t: (b * nT + tmap(t), 0)),
            pl.BlockSpec((LANES, tb), lambda b, kh, t: (b, tmap(t))),
        ],
        out_specs=pl.BlockSpec((tb, 2 * G * DN_DV), lambda b, kh, t: (b * nT + tmap(t), kh)),
        out_shape=jax.ShapeDtypeStruct((B * T, DN_VW), F32),
        scratch_shapes=[pltpu.VMEM((2 * G, DN_DK, DN_DV), F32)],
        compiler_params=_params("parallel", "parallel", "arbitrary"),
        name="dn_chunk_bwd" if rev else "dn_chunk_fwd",
    )(conv_out, conv_out, conv_out, gates, gates_t)


def _dn_gnorm_kernel(of_ref, ob_ref, z_ref, g_ref, o_ref, *, tc):
    g = g_ref[...]
    for s in range(tc // LANES):
        cols = slice(s * LANES, (s + 1) * LANES)
        o = of_ref[:, cols] + ob_ref[:, cols]
        o = o * lax.rsqrt(jnp.mean(o * o, axis=-1, keepdims=True) + RMS_EPS) * g
        z = z_ref[:, cols].astype(F32)
        o_ref[:, cols] = (o * (z * _sigmoid(z))).astype(o_ref.dtype)


def dn_gnorm(o_f, o_b, proj, norm_g, tm=512, tc=512):
    M = o_f.shape[0]
    tm = min(tm, M)
    z_off = DN_CONV_DIM // tc
    return pl.pallas_call(
        functools.partial(_dn_gnorm_kernel, tc=tc),
        grid=(M // tm, DN_VW // tc),
        in_specs=[
            pl.BlockSpec((tm, tc), lambda i, j: (i, j)),
            pl.BlockSpec((tm, tc), lambda i, j: (i, j)),
            pl.BlockSpec((tm, tc), lambda i, j: (i, z_off + j)),
            pl.BlockSpec((1, DN_DV), lambda i, j: (0, 0)),
        ],
        out_specs=pl.BlockSpec((tm, tc), lambda i, j: (i, j)),
        out_shape=jax.ShapeDtypeStruct((M, DN_VW), BF16),
        compiler_params=_params("parallel", "parallel"),
        name="dn_gnorm",
    )(o_f, o_b, proj, norm_g.reshape(1, DN_DV).astype(F32))


def _lambda_init(layer_idx):
    return 0.8 - 0.6 * math.exp(-0.3 * layer_idx)


def _trunk(x, p):
    B, T, D = x.shape
    xf = x.reshape(B * T, D)
    h = rmsnorm(xf, p["norm_g"][0])
    for i in range(DEPTH):
        j = i // 2
        if i % 2 == 0:
            proj = matmul(h, p["attn_w_in"][j], BF16)
            a = diff_attention(proj, p["attn_lambda"][j], p["attn_subln_g"][j], B, T, _lambda_init(i))
            w_out = p["attn_w_out"][j]
        else:
            proj = matmul(h, p["dn_w_main"][j], BF16)
            ab = matmul(h, p["dn_w_ab"][j], F32)
            conv_out = dn_conv(proj, p["dn_conv_w"][j], B, T)
            gates, gates_t = dn_gates(ab, p["dn_a_log_fwd"][j], p["dn_dt_bias_fwd"][j],
                                      p["dn_a_log_bwd"][j], p["dn_dt_bias_bwd"][j], B, T)
            o_f = dn_chunk(conv_out, gates, gates_t, B, T, rev=False)
            o_b = dn_chunk(conv_out, gates, gates_t, B, T, rev=True)
            a = dn_gnorm(o_f, o_b, proj, p["dn_norm_g"][j])
            w_out = p["dn_w_out"][j]
        if i + 1 < DEPTH:
            xf, h = outproj(a, w_out, xf, p["norm_g"][i + 1], last=False)
        else:
            xf = outproj(a, w_out, xf, p["final_norm_g"], last=True)
    return xf.reshape(B, T, D)


def kernel(x_prompt, x_sample, norm_g, attn_w_in, attn_lambda, attn_subln_g, attn_w_out, dn_w_in, dn_conv_w, dn_a_log_fwd, dn_dt_bias_fwd, dn_a_log_bwd, dn_dt_bias_bwd, dn_norm_g, dn_w_out, final_norm_g):
    p = {
        "norm_g": norm_g.astype(F32),
        "attn_w_in": attn_w_in.astype(BF16),
        "attn_lambda": attn_lambda,
        "attn_subln_g": attn_subln_g,
        "attn_w_out": attn_w_out.astype(BF16),
        "dn_w_main": dn_w_in[:, :, :DN_MAIN].astype(BF16),
        "dn_w_ab": dn_w_in[:, :, DN_MAIN:].astype(BF16),
        "dn_conv_w": dn_conv_w,
        "dn_a_log_fwd": dn_a_log_fwd,
        "dn_dt_bias_fwd": dn_dt_bias_fwd,
        "dn_a_log_bwd": dn_a_log_bwd,
        "dn_dt_bias_bwd": dn_dt_bias_bwd,
        "dn_norm_g": dn_norm_g,
        "dn_w_out": dn_w_out.astype(BF16),
        "final_norm_g": final_norm_g.astype(F32),
    }
    return (_trunk(x_prompt, p), _trunk(x_sample, p))
```

```python
import functools
import math

import jax
import jax.numpy as jnp
from jax import lax
from jax.experimental import pallas as pl
from jax.experimental.pallas import tpu as pltpu

F32 = jnp.float32
BF16 = jnp.bfloat16

D_MODEL = 2048
DEPTH = 4
RMS_EPS = 1e-6
L2_EPS = 1e-6

ATTN_HEADS = 8
ATTN_HEAD_DIM = 128
ATTN_V_DIM = 256
ATTN_IN = 8192

DN_K_HEADS = 16
DN_V_HEADS = 32
DN_DK = 128
DN_DV = 128
DN_KW = 2048
DN_VW = 4096
DN_CONV_DIM = 8192
DN_MAIN = DN_CONV_DIM + DN_VW
CONV_W = 5
CHUNK = 64

LANES = 128
VMEM_LIMIT = 56 * 1024 * 1024
NEG_BIG = -1e30

NT_DIMS = (((1,), (1,)), ((), ()))
TN_DIMS = (((0,), (0,)), ((), ()))


def _params(*sem):
    return pltpu.CompilerParams(dimension_semantics=sem, vmem_limit_bytes=VMEM_LIMIT)


def _sigmoid(x):
    return 1.0 / (1.0 + jnp.exp(-x))


def _rmsnorm_kernel(x_ref, g_ref, o_ref):
    x = x_ref[...]
    y = x * lax.rsqrt(jnp.mean(x * x, axis=-1, keepdims=True) + RMS_EPS)
    o_ref[...] = (y * g_ref[...]).astype(o_ref.dtype)


def rmsnorm(x, g, tm=512):
    M, D = x.shape
    tm = min(tm, M)
    return pl.pallas_call(
        _rmsnorm_kernel,
        grid=(M // tm,),
        in_specs=[pl.BlockSpec((tm, D), lambda i: (i, 0)), pl.BlockSpec((1, D), lambda i: (0, 0))],
        out_specs=pl.BlockSpec((tm, D), lambda i: (i, 0)),
        out_shape=jax.ShapeDtypeStruct((M, D), BF16),
        compiler_params=_params("parallel"),
        name="rmsnorm",
    )(x, g.reshape(1, D))


def _matmul_kernel(a_ref, w_ref, o_ref):
    o_ref[...] = jnp.dot(a_ref[...], w_ref[...], preferred_element_type=F32).astype(o_ref.dtype)


def matmul(a, w, out_dtype, tm=1024, tn=1024):
    M, K = a.shape
    N = w.shape[1]
    tm, tn = min(tm, M), min(tn, N)
    return pl.pallas_call(
        _matmul_kernel,
        grid=(M // tm, N // tn),
        in_specs=[pl.BlockSpec((tm, K), lambda i, j: (i, 0)), pl.BlockSpec((K, tn), lambda i, j: (0, j))],
        out_specs=pl.BlockSpec((tm, tn), lambda i, j: (i, j)),
        out_shape=jax.ShapeDtypeStruct((M, N), out_dtype),
        compiler_params=_params("parallel", "parallel"),
        name="in_proj",
    )(a, w)


def _outproj_kernel(a_ref, w_ref, x_ref, g_ref, *out_refs, last):
    x_new = x_ref[...] + jnp.dot(a_ref[...], w_ref[...], preferred_element_type=F32)
    y = x_new * lax.rsqrt(jnp.mean(x_new * x_new, axis=-1, keepdims=True) + RMS_EPS) * g_ref[...]
    if last:
        out_refs[0][...] = y
    else:
        out_refs[0][...] = x_new
        out_refs[1][...] = y.astype(BF16)


def outproj(a, w, x, g, last, tm=256):
    M, K = a.shape
    D = w.shape[1]
    tm = min(tm, M)
    row = lambda i: (i, 0)
    if last:
        out_shape = jax.ShapeDtypeStruct((M, D), F32)
        out_specs = pl.BlockSpec((tm, D), row)
    else:
        out_shape = (jax.ShapeDtypeStruct((M, D), F32), jax.ShapeDtypeStruct((M, D), BF16))
        out_specs = (pl.BlockSpec((tm, D), row), pl.BlockSpec((tm, D), row))
    return pl.pallas_call(
        functools.partial(_outproj_kernel, last=last),
        grid=(M // tm,),
        in_specs=[
            pl.BlockSpec((tm, K), row),
            pl.BlockSpec((K, D), lambda i: (0, 0), pipeline_mode=pl.Buffered(1)),
            pl.BlockSpec((tm, D), row),
            pl.BlockSpec((1, D), lambda i: (0, 0)),
        ],
        out_specs=out_specs,
        out_shape=out_shape,
        compiler_params=_params("parallel"),
        name="out_proj",
    )(a, w, x, g.reshape(1, D))


ATTN_TQ = 256
ATTN_SUB = 256
LOG2E = 1.4426950408889634


def _split3(x):
    hi = x.astype(BF16)
    r = x - hi.astype(F32)
    mid = r.astype(BF16)
    lo = (r - mid.astype(F32)).astype(BF16)
    return hi, mid, lo


def _split3_f32(x):
    hi, mid, lo = _split3(x)
    return hi.astype(F32), mid.astype(F32), lo.astype(F32)


def _attn_prep_kernel(slopes_ref, k_ref, v_ref, kaug_ref, vt_ref, *, tp):
    h = pl.program_id(1)
    i = pl.program_id(2)
    shape = (tp, LANES)
    lane = lax.broadcasted_iota(jnp.int32, shape, 1)
    pos = lax.broadcasted_iota(jnp.int32, shape, 0) + i * tp
    hi, mid, lo = _split3_f32(jnp.full(shape, slopes_ref[h] * LOG2E, F32))
    j_lo = (pos % ATTN_SUB).astype(F32)
    j_hi = (((pos // ATTN_SUB) % 2) * ATTN_SUB).astype(F32)
    ext = jnp.where(lane == 0, -hi, jnp.where(lane == 1, -mid, jnp.where(lane == 2, -lo,
          jnp.where(lane < 6, j_lo, jnp.where(lane < 9, j_hi, 0.0))))).astype(BF16)
    k = k_ref[...]
    kaug_ref[:, 0 * LANES:1 * LANES] = k[:, :ATTN_HEAD_DIM]
    kaug_ref[:, 1 * LANES:2 * LANES] = ext
    kaug_ref[:, 2 * LANES:3 * LANES] = k[:, ATTN_HEAD_DIM:]
    kaug_ref[:, 3 * LANES:4 * LANES] = ext
    vt_ref[0] = v_ref[...].astype(F32).T.astype(BF16)


def _attn_kernel(slopes_ref, lam_ref, g_ref, q_ref, kaug_ref, vt_ref, gate_ref, o_ref,
                 qv_ref, s_ref, p_ref, acc_ref, *, tq, tk, nk, lambda_init):
    h = pl.program_id(1)
    i = pl.program_id(2)
    c_slope = slopes_ref[h] * LOG2E
    i0 = i * tq
    q = q_ref[...].astype(F32) * (ATTN_HEAD_DIM ** -0.5 * LOG2E)
    q1 = q[:, :ATTN_HEAD_DIM].astype(BF16)
    q2 = q[:, ATTN_HEAD_DIM:].astype(BF16)
    shape = (tq, LANES)
    lane = lax.broadcasted_iota(jnp.int32, shape, 1)
    row = lax.broadcasted_iota(jnp.int32, shape, 0).astype(F32)
    hi, mid, lo = _split3_f32(jnp.full(shape, c_slope, F32))
    c_part = jnp.where(lane % 3 == 0, hi, jnp.where(lane % 3 == 1, mid, lo))
    ext = jnp.where(lane < 3, row, jnp.where(lane < 9, c_part, 0.0))
    for mode, e in ((0, ext), (1, -ext), (2, jnp.zeros(shape, F32))):
        eb = e.astype(BF16)
        qv_ref[mode, 0] = jnp.concatenate([q1, eb], axis=1)
        qv_ref[mode, 1] = jnp.concatenate([q2, eb], axis=1)
    c_diag = i0 // tk
    acc_ref[...] = jnp.zeros(acc_ref.shape, F32)

    def chunk_of(u):
        if isinstance(u, int) and u == nk - 1:
            return c_diag
        return u + (u >= c_diag).astype(jnp.int32)

    def stage_scores(u, par, carry):
        c = chunk_of(u)
        is_diag = isinstance(u, int) and u == nk - 1
        kc = kaug_ref[pl.ds(pl.multiple_of(c * tk, tk), tk), :]
        mode = 2 if is_diag else (c > c_diag).astype(jnp.int32)
        if is_diag:
            pos_k = c * tk + lax.broadcasted_iota(jnp.int32, (tk, 1), 0)
            pos_q = i0 + lax.broadcasted_iota(jnp.int32, (1, tq), 1)
            bias = -c_slope * jnp.abs(pos_k - pos_q).astype(F32)
        out = list(carry)
        for mi in range(2):
            kk = kc[:, 2 * mi * LANES:(2 * mi + 2) * LANES]
            s = lax.dot_general(kk, qv_ref[mode, mi], NT_DIMS, preferred_element_type=F32)
            if is_diag:
                s = s + bias
            s_ref[par, mi] = s
            out[4 * mi + 3] = jnp.max(s, axis=0, keepdims=True)
        return tuple(out)

    def stage_softmax(u, par, carry):
        c = chunk_of(u)
        if isinstance(u, int) and u == nk - 1:
            const = 0.0
        else:
            const = -c_slope * jnp.maximum(i0 - c * tk, c * tk - i0).astype(F32)
        out = list(carry)
        for mi in range(2):
            m, l, _, cmax = carry[4 * mi:4 * mi + 4]
            m_new = jnp.maximum(m, cmax + const)
            p = jnp.exp2(s_ref[par, mi] - (m_new - const))
            p_ref[par, mi] = p.astype(BF16)
            out[4 * mi:4 * mi + 3] = [m_new, jnp.exp2(m - m_new) * l + jnp.sum(p, axis=0, keepdims=True),
                                      jnp.exp2(m - m_new)]
        return tuple(out)

    def stage_pv(u, par, carry):
        vtc = vt_ref[chunk_of(u)]
        for mi in range(2):
            alpha = carry[4 * mi + 2]
            acc_ref[mi] = alpha * acc_ref[mi] + jnp.dot(vtc, p_ref[par, mi], preferred_element_type=F32)

    def tick(t, par, carry, static=False):
        if not static or 0 <= t - 2 < nk:
            stage_pv(t - 2, par, carry)
        if not static or 0 <= t - 1 < nk:
            carry = stage_softmax(t - 1, 1 - par, carry)
        if not static or 0 <= t < nk:
            carry = stage_scores(t, par, carry)
        return carry

    neg = jnp.full((1, tq), NEG_BIG, F32)
    zero = jnp.zeros((1, tq), F32)
    carry = (neg, zero, zero, neg) * 2
    n_pro = min(2, nk)
    for t in range(n_pro):
        carry = tick(t, t % 2, carry, static=True)
    n_pairs = max(nk - 1 - n_pro, 0) // 2

    def pair(j, cr):
        t = n_pro + 2 * j
        cr = tick(t, 0, cr)
        return tick(t + 1, 1, cr)

    carry = lax.fori_loop(0, n_pairs, pair, carry)
    for t in range(n_pro + 2 * n_pairs, nk + 2):
        carry = tick(t, t % 2, carry, static=True)
    l1, l2 = carry[1], carry[5]

    lv = lam_ref[...]
    lam = (jnp.exp(jnp.sum(lv[0:1] * lv[1:2], axis=-1, keepdims=True))
           - jnp.exp(jnp.sum(lv[2:3] * lv[3:4], axis=-1, keepdims=True)) + lambda_init)
    o = (acc_ref[0] * (1.0 / l1) - lam * (acc_ref[1] * (1.0 / l2))).T
    o = o * lax.rsqrt(jnp.mean(o * o, axis=-1, keepdims=True) + RMS_EPS) * g_ref[...] * (1.0 - lambda_init)
    gate = gate_ref[...].astype(F32)
    o_ref[...] = (o * (gate * _sigmoid(gate))).astype(o_ref.dtype)


def diff_attention(proj, lam_vec, subln_g, B, T, lambda_init, tk=512):
    tq, tk = min(ATTN_TQ, T), min(tk, T)
    nq, nk = T // tq, T // tk
    H = ATTN_HEADS
    slopes = jnp.exp2(-8.0 * jnp.arange(1, H + 1, dtype=F32) / H)
    smem = pl.BlockSpec(memory_space=pltpu.SMEM)
    kaug, vt = pl.pallas_call(
        functools.partial(_attn_prep_kernel, tp=tk),
        grid=(B, H, nk),
        in_specs=[
            smem,
            pl.BlockSpec((tk, ATTN_V_DIM), lambda b, h, i: (b * nk + i, H + h)),
            pl.BlockSpec((tk, ATTN_V_DIM), lambda b, h, i: (b * nk + i, 2 * H + h)),
        ],
        out_specs=(pl.BlockSpec((tk, 4 * LANES), lambda b, h, i: (b * nk + i, h)),
                   pl.BlockSpec((1, ATTN_V_DIM, tk), lambda b, h, i: ((b * H + h) * nk + i, 0, 0))),
        out_shape=(jax.ShapeDtypeStruct((B * T, H * 4 * LANES), BF16),
                   jax.ShapeDtypeStruct((B * H * nk, ATTN_V_DIM, tk), BF16)),
        compiler_params=_params("parallel", "parallel", "parallel"),
        name="attn_prep",
    )(slopes, proj, proj)
    return pl.pallas_call(
        functools.partial(_attn_kernel, tq=tq, tk=tk, nk=nk, lambda_init=lambda_init),
        grid=(B, H, nq),
        in_specs=[
            smem,
            pl.BlockSpec((4, ATTN_HEAD_DIM), lambda b, h, i: (0, 0)),
            pl.BlockSpec((1, ATTN_V_DIM), lambda b, h, i: (0, 0)),
            pl.BlockSpec((tq, ATTN_V_DIM), lambda b, h, i: (b * nq + i, h)),
            pl.BlockSpec((T, 4 * LANES), lambda b, h, i: (b, h), pipeline_mode=pl.Buffered(1)),
            pl.BlockSpec((nk, ATTN_V_DIM, tk), lambda b, h, i: (b * H + h, 0, 0), pipeline_mode=pl.Buffered(1)),
            pl.BlockSpec((tq, ATTN_V_DIM), lambda b, h, i: (b * nq + i, 3 * H + h)),
        ],
        out_specs=pl.BlockSpec((tq, ATTN_V_DIM), lambda b, h, i: (b * nq + i, h)),
        out_shape=jax.ShapeDtypeStruct((B * T, H * ATTN_V_DIM), BF16),
        scratch_shapes=[pltpu.VMEM((3, 2, tq, 2 * LANES), BF16),
                        pltpu.VMEM((2, 2, tk, tq), F32), pltpu.VMEM((2, 2, tk, tq), BF16),
                        pltpu.VMEM((2, ATTN_V_DIM, tq), F32)],
        compiler_params=_params("parallel", "parallel", "arbitrary"),
        name="diff_attn",
    )(slopes, lam_vec.astype(F32), subln_g.reshape(1, ATTN_V_DIM).astype(F32), proj, kaug, vt, proj)


def _dn_conv_kernel(x_ref, prev_ref, next_ref, w_ref, o_ref, ext_ref, *, tT, tc, nT, n_qk_blocks):
    i = pl.program_id(1)
    j = pl.program_id(2)
    halo = prev_ref.shape[0]
    prev = prev_ref[...].astype(F32)[halo - 8:, :]
    nxt = next_ref[...].astype(F32)[:8, :]
    ext_ref[0:8, :] = jnp.where(i > 0, prev, 0.0)
    ext_ref[8:8 + tT, :] = x_ref[...].astype(F32)
    ext_ref[8 + tT:16 + tT, :] = jnp.where(i < nT - 1, nxt, 0.0)
    w = w_ref[...]
    y = jnp.zeros((tT, tc), F32)
    for tap in range(CONV_W):
        y = y + w[tap:tap + 1, :] * ext_ref[pl.ds(8 + tap - CONV_W // 2, tT), :]
    y = y * _sigmoid(y)

    @pl.when(j < n_qk_blocks)
    def _():
        for g in range(tc // LANES):
            seg = y[:, g * LANES:(g + 1) * LANES]
            seg = seg * lax.rsqrt(jnp.sum(seg * seg, axis=-1, keepdims=True) + L2_EPS)
            o_ref[:, g * LANES:(g + 1) * LANES] = seg.astype(o_ref.dtype)

    @pl.when(j >= n_qk_blocks)
    def _():
        o_ref[...] = y.astype(o_ref.dtype)


def dn_conv(proj, conv_w, B, T, tT=512, tc=512):
    tT = min(tT, T)
    nT = T // tT
    halo = 16
    hb = tT // halo
    n_halo_blocks = B * T // halo
    return pl.pallas_call(
        functools.partial(_dn_conv_kernel, tT=tT, tc=tc, nT=nT, n_qk_blocks=2 * DN_KW // tc),
        grid=(B, nT, DN_CONV_DIM // tc),
        in_specs=[
            pl.BlockSpec((tT, tc), lambda b, i, j: (b * nT + i, j)),
            pl.BlockSpec((halo, tc), lambda b, i, j: (jnp.maximum((b * nT + i) * hb - 1, 0), j)),
            pl.BlockSpec((halo, tc), lambda b, i, j: (jnp.minimum((b * nT + i + 1) * hb, n_halo_blocks - 1), j)),
            pl.BlockSpec((CONV_W, tc), lambda b, i, j: (0, j)),
        ],
        out_specs=pl.BlockSpec((tT, tc), lambda b, i, j: (b * nT + i, j)),
        out_shape=jax.ShapeDtypeStruct((B * T, DN_CONV_DIM), BF16),
        scratch_shapes=[pltpu.VMEM((tT + 16, tc), F32)],
        compiler_params=_params("parallel", "parallel", "parallel"),
        name="dn_conv",
    )(proj, proj, proj, conv_w.astype(F32))


def _dn_gates_kernel(ab_ref, alog_ref, dtb_ref, o_ref, ot_ref, *, tT):
    ab = ab_ref[...]
    x = ab + dtb_ref[...]
    softplus = jnp.maximum(x, 0.0) + jnp.log1p(jnp.exp(-jnp.abs(x)))
    g = -jnp.exp(alog_ref[...]) * softplus
    beta = _sigmoid(ab)
    r = lax.broadcasted_iota(jnp.int32, (tT, tT), 0)
    s = lax.broadcasted_iota(jnp.int32, (tT, tT), 1)
    same = (r // CHUNK) == (s // CHUNK)
    lower = jnp.where(same & (r >= s), 1.0, 0.0).astype(BF16)
    upper = jnp.where(same & (r <= s), 1.0, 0.0).astype(BF16)
    parts = _split3(g)
    cum_f = sum(jnp.dot(lower, p, preferred_element_type=F32) for p in parts)
    cum_b = sum(jnp.dot(upper, p, preferred_element_type=F32) for p in parts)
    lane = lax.broadcasted_iota(jnp.int32, (tT, LANES), 1)
    is_beta = (lane // DN_V_HEADS) % 2 == 1
    out = jnp.where(is_beta, beta, jnp.where(lane < 2 * DN_V_HEADS, cum_f, cum_b))
    o_ref[...] = out
    ot_ref[...] = out.T


def dn_gates(ab, a_log_f, dt_bias_f, a_log_b, dt_bias_b, B, T, tT=256):
    tT = min(tT, T)
    nT = T // tT
    zeros = jnp.zeros((DN_V_HEADS,), F32)
    alog = jnp.concatenate([a_log_f.astype(F32), zeros, a_log_b.astype(F32), zeros]).reshape(1, LANES)
    dtb = jnp.concatenate([dt_bias_f.astype(F32), zeros, dt_bias_b.astype(F32), zeros]).reshape(1, LANES)
    return pl.pallas_call(
        functools.partial(_dn_gates_kernel, tT=tT),
        grid=(B, nT),
        in_specs=[
            pl.BlockSpec((tT, LANES), lambda b, i: (b * nT + i, 0)),
            pl.BlockSpec((1, LANES), lambda b, i: (0, 0)),
            pl.BlockSpec((1, LANES), lambda b, i: (0, 0)),
        ],
        out_specs=(pl.BlockSpec((tT, LANES), lambda b, i: (b * nT + i, 0)),
                   pl.BlockSpec((LANES, tT), lambda b, i: (b, i))),
        out_shape=(jax.ShapeDtypeStruct((B * T, LANES), F32), jax.ShapeDtypeStruct((B * LANES, T), F32)),
        compiler_params=_params("parallel", "parallel"),
        name="dn_gates",
    )(ab, alog, dtb)


DN_GROUP = 2


def _dn_chunk_kernel(q_ref, k_ref, v_ref, gt_ref, gtt_ref, o_ref, s_ref, *, nc, rev):
    kh0 = pl.program_id(1) * DN_GROUP
    t = pl.program_id(2)

    @pl.when(t == 0)
    def _():
        s_ref[...] = jnp.zeros(s_ref.shape, F32)

    scale = DN_DK ** -0.5
    r = lax.broadcasted_iota(jnp.int32, (CHUNK, CHUNK), 0)
    c = lax.broadcasted_iota(jnp.int32, (CHUNK, CHUNK), 1)
    incl = (r <= c) if rev else (r >= c)
    strict = (r < c) if rev else (r > c)
    lane = lax.broadcasted_iota(jnp.int32, (CHUNK, LANES), 1)
    g_off = 2 * DN_V_HEADS if rev else 0
    last_row = 0 if rev else CHUNK - 1
    dot = functools.partial(jnp.dot, preferred_element_type=F32)

    heads = [(g, hh) for g in range(DN_GROUP) for hh in range(2)]
    g_rows = {(g, hh): gtt_ref[pl.ds(g_off + 2 * (kh0 + g) + hh, 1), :] for g, hh in heads}
    order = list(range(nc - 1, -1, -1) if rev else range(nc))

    inst = []
    for ci in order:
        rows = slice(ci * CHUNK, (ci + 1) * CHUNK)
        gt = gt_ref[rows, :]
        for g in range(DN_GROUP):
            qc = q_ref[rows, g * DN_DK:(g + 1) * DN_DK]
            kc = k_ref[rows, g * DN_DK:(g + 1) * DN_DK]
            kf = kc.astype(F32)
            qf = qc.astype(F32)
            gram = lax.dot_general(jnp.concatenate([kc, qc], axis=0), kc, NT_DIMS, preferred_element_type=F32)
            kk = gram[:CHUNK]
            qk = gram[CHUNK:]
            for hh in range(2):
                hv = 2 * (kh0 + g) + hh
                gc = jnp.sum(jnp.where(lane == g_off + hv, gt, 0.0), axis=-1, keepdims=True)
                beta = jnp.sum(jnp.where(lane == g_off + DN_V_HEADS + hv, gt, 0.0), axis=-1, keepdims=True)
                g_row = g_rows[(g, hh)][:, rows]
                decay = jnp.exp(jnp.where(incl, gc - g_row, -jnp.inf))
                eg = jnp.exp(gc)
                g_last = gc[last_row:last_row + 1, :]
                col = (2 * g + hh) * DN_DV
                vb = v_ref[rows, col:col + DN_DV].astype(F32) * beta
                inst.append(dict(
                    rows=rows, col=col, chain=2 * g + hh,
                    m=jnp.where(strict, -(kk * beta * decay), 0.0),
                    a=(qk * scale * decay).astype(BF16),
                    rhs=jnp.concatenate([vb, kf * (beta * eg)], axis=1),
                    kd=(kf * jnp.exp(g_last - gc)).astype(BF16),
                    qd=qf * (scale * eg),
                    egl=jnp.exp(g_last)))

    ms = [d["m"] for d in inst]
    ys = list(ms)
    for _ in range(5):
        mbs = [m.astype(BF16) for m in ms]
        ms = [dot(mb, mb) for mb in mbs]
        ys = [y + m + dot(y.astype(BF16), m.astype(BF16)) for y, m in zip(ys, ms)]

    uws = [(d["rhs"] + dot(y.astype(BF16), d["rhs"].astype(BF16))).astype(BF16) for d, y in zip(inst, ys)]
    kd_uws = [lax.dot_general(d["kd"], uw, TN_DIMS, preferred_element_type=F32) for d, uw in zip(inst, uws)]
    a_uws = [dot(d["a"], uw) for d, uw in zip(inst, uws)]
    for d, kd_uw, a_uw in zip(inst, kd_uws, a_uws):
        d["lhs"] = jnp.concatenate([kd_uw[:, DN_DV:], d["qd"] - a_uw[:, DN_DV:]], axis=0).astype(BF16)
        d["r"] = kd_uw[:, :DN_DV]
        d["o"] = a_uw[:, :DN_DV]

    states = [s_ref[j] for j in range(2 * DN_GROUP)]
    for d in inst:
        S = states[d["chain"]]
        res = dot(d["lhs"], S.astype(BF16))
        states[d["chain"]] = d["egl"] * S - res[:DN_DK] + d["r"]
        o_ref[d["rows"], d["col"]:d["col"] + DN_DV] = res[DN_DK:] + d["o"]
    for j in range(2 * DN_GROUP):
        s_ref[j] = states[j]


def dn_chunk(conv_out, gates, gates_t, B, T, rev, tb=512):
    tb = min(tb, T)
    nT = T // tb
    G = DN_GROUP
    n_groups = DN_K_HEADS // G
    tmap = (lambda t: nT - 1 - t) if rev else (lambda t: t)
    return pl.pallas_call(
        functools.partial(_dn_chunk_kernel, nc=tb // CHUNK, rev=rev),
        grid=(B, n_groups, nT),
        in_specs=[
            pl.BlockSpec((tb, G * DN_DK), lambda b, kh, t: (b * nT + tmap(t), kh)),
            pl.BlockSpec((tb, G * DN_DK), lambda b, kh, t: (b * nT + tmap(t), n_groups + kh)),
            pl.BlockSpec((tb, 2 * G * DN_DV), lambda b, kh, t: (b * nT + tmap(t), n_groups + kh)),
            pl.BlockSpec((tb, LANES), lambda b, kh, t: (b * nT + tmap(t), 0)),
            pl.BlockSpec((LANES, tb), lambda b, kh, t: (b, tmap(t))),
        ],
        out_specs=pl.BlockSpec((tb, 2 * G * DN_DV), lambda b, kh, t: (b * nT + tmap(t), kh)),
        out_shape=jax.ShapeDtypeStruct((B * T, DN_VW), F32),
        scratch_shapes=[pltpu.VMEM((2 * G, DN_DK, DN_DV), F32)],
        compiler_params=_params("parallel", "parallel", "arbitrary"),
        name="dn_chunk_bwd" if rev else "dn_chunk_fwd",
    )(conv_out, conv_out, conv_out, gates, gates_t)


def _dn_gnorm_kernel(of_ref, ob_ref, z_ref, g_ref, o_ref, *, tc):
    g = g_ref[...]
    for s in range(tc // LANES):
        cols = slice(s * LANES, (s + 1) * LANES)
        o = of_ref[:, cols] + ob_ref[:, cols]
        o = o * lax.rsqrt(jnp.mean(o * o, axis=-1, keepdims=True) + RMS_EPS) * g
        z = z_ref[:, cols].astype(F32)
        o_ref[:, cols] = (o * (z * _sigmoid(z))).astype(o_ref.dtype)


def dn_gnorm(o_f, o_b, proj, norm_g, tm=512, tc=512):
    M = o_f.shape[0]
    tm = min(tm, M)
    z_off = DN_CONV_DIM // tc
    return pl.pallas_call(
        functools.partial(_dn_gnorm_kernel, tc=tc),
        grid=(M // tm, DN_VW // tc),
        in_specs=[
            pl.BlockSpec((tm, tc), lambda i, j: (i, j)),
            pl.BlockSpec((tm, tc), lambda i, j: (i, j)),
            pl.BlockSpec((tm, tc), lambda i, j: (i, z_off + j)),
            pl.BlockSpec((1, DN_DV), lambda i, j: (0, 0)),
        ],
        out_specs=pl.BlockSpec((tm, tc), lambda i, j: (i, j)),
        out_shape=jax.ShapeDtypeStruct((M, DN_VW), BF16),
        compiler_params=_params("parallel", "parallel"),
        name="dn_gnorm",
    )(o_f, o_b, proj, norm_g.reshape(1, DN_DV).astype(F32))


def _lambda_init(layer_idx):
    return 0.8 - 0.6 * math.exp(-0.3 * layer_idx)


def _trunk(x, p):
    B, T, D = x.shape
    xf = x.reshape(B * T, D)
    h = rmsnorm(xf, p["norm_g"][0])
    for i in range(DEPTH):
        j = i // 2
        if i % 2 == 0:
            proj = matmul(h, p["attn_w_in"][j], BF16)
            a = diff_attention(proj, p["attn_lambda"][j], p["attn_subln_g"][j], B, T, _lambda_init(i))
            w_out = p["attn_w_out"][j]
        else:
            proj = matmul(h, p["dn_w_main"][j], BF16)
            ab = matmul(h, p["dn_w_ab"][j], F32)
            conv_out = dn_conv(proj, p["dn_conv_w"][j], B, T)
            gates, gates_t = dn_gates(ab, p["dn_a_log_fwd"][j], p["dn_dt_bias_fwd"][j],
                                      p["dn_a_log_bwd"][j], p["dn_dt_bias_bwd"][j], B, T)
            o_f = dn_chunk(conv_out, gates, gates_t, B, T, rev=False)
            o_b = dn_chunk(conv_out, gates, gates_t, B, T, rev=True)
            a = dn_gnorm(o_f, o_b, proj, p["dn_norm_g"][j])
            w_out = p["dn_w_out"][j]
        if i + 1 < DEPTH:
            xf, h = outproj(a, w_out, xf, p["norm_g"][i + 1], last=False)
        else:
            xf = outproj(a, w_out, xf, p["final_norm_g"], last=True)
    return xf.reshape(B, T, D)


def kernel(x_prompt, x_sample, norm_g, attn_w_in, attn_lambda, attn_subln_g, attn_w_out, dn_w_in, dn_conv_w, dn_a_log_fwd, dn_dt_bias_fwd, dn_a_log_bwd, dn_dt_bias_bwd, dn_norm_g, dn_w_out, final_norm_g):
    p = {
        "norm_g": norm_g.astype(F32),
        "attn_w_in": attn_w_in.astype(BF16),
        "attn_lambda": attn_lambda,
        "attn_subln_g": attn_subln_g,
        "attn_w_out": attn_w_out.astype(BF16),
        "dn_w_main": dn_w_in[:, :, :DN_MAIN].astype(BF16),
        "dn_w_ab": dn_w_in[:, :, DN_MAIN:].astype(BF16),
        "dn_conv_w": dn_conv_w,
        "dn_a_log_fwd": dn_a_log_fwd,
        "dn_dt_bias_fwd": dn_dt_bias_fwd,
        "dn_a_log_bwd": dn_a_log_bwd,
        "dn_dt_bias_bwd": dn_dt_bias_bwd,
        "dn_norm_g": dn_norm_g,
        "dn_w_out": dn_w_out.astype(BF16),
        "final_norm_g": final_norm_g.astype(F32),
    }
    return (_trunk(x_prompt, p), _trunk(x_sample, p))
```
